```python
import jax, jax.numpy as jnp
from jax import lax
import numpy as np

D_MODEL = 2048
BATCH = 8
SEQ = 4096
DEPTH = 1

CHUNK = 64
EPS = 1e-6
D_MIX = D_MODEL
GMLP_BLOCK = 128
GMLP_WIDTH = D_MIX // 2
GMLP_GROUPS = 8
GMLP_GROUP_DIM = GMLP_WIDTH // GMLP_GROUPS
ATT_WIDTH = D_MIX - GMLP_WIDTH
N_HEADS = 16
HEAD_DIM = ATT_WIDTH // N_HEADS
KV_LATENT = 256
IDX_HEADS = 8
IDX_DIM = 64
TOPK_MAX = 256
Q_BLOCK = 128
ROPE_THETA = 10000.0
IN_SPLIT_SIZES = (GMLP_WIDTH, GMLP_WIDTH, ATT_WIDTH, KV_LATENT, IDX_HEADS * IDX_DIM, IDX_DIM, IDX_HEADS)
IN_COLS = sum(IN_SPLIT_SIZES)
N_EXPERT_GROUPS = 4
EXPERTS_PER_GROUP = 8
N_EXPERTS = N_EXPERT_GROUPS * EXPERTS_PER_GROUP
TOP_K_IN_GROUP = 2
D_EXPERT = 1024
MOE_BLOCK = 128

kernel_name = "hybrid_gmlp_dsa_hmoe_block"


def rms_norm(x, g):
    xf = x.astype(jnp.float32)
    y = xf * lax.rsqrt(jnp.mean(xf * xf, axis=-1, keepdims=True) + EPS)
    return (y * g).astype(x.dtype)


def layer_norm(x, g, b):
    xf = x.astype(jnp.float32)
    mu = jnp.mean(xf, axis=-1, keepdims=True)
    var = jnp.mean(jnp.square(xf - mu), axis=-1, keepdims=True)
    return ((xf - mu) * lax.rsqrt(var + EPS) * g + b).astype(x.dtype)


def rope(x, pos):
    half = x.shape[-1] // 2
    freqs = ROPE_THETA ** (-jnp.arange(half, dtype=jnp.float32) / half)
    ang = pos.astype(jnp.float32)[..., None] * freqs
    cos = jnp.cos(ang)[:, :, None, :]
    sin = jnp.sin(ang)[:, :, None, :]
    x1, x2 = x[..., :half], x[..., half:]
    return jnp.concatenate([x1 * cos - x2 * sin, x1 * sin + x2 * cos], axis=-1).astype(x.dtype)


def gmlp_spatial_gate(u, v, v_norm_g, v_norm_b, w_sp, b_sp):
    B, S, _ = u.shape
    nb = S // GMLP_BLOCK
    v = v.reshape(B, nb, GMLP_BLOCK, GMLP_GROUPS, GMLP_GROUP_DIM)
    v = layer_norm(v, v_norm_g, v_norm_b)
    pos = jnp.arange(GMLP_BLOCK)
    mask = (pos[:, None] // CHUNK) >= (pos[None, :] // CHUNK)
    w = jnp.where(mask[None], w_sp, jnp.zeros_like(w_sp))
    z = jnp.einsum('gij,bnjgc->bnigc', w, v) + b_sp.T[None, None, :, :, None]
    return u * z.reshape(B, S, GMLP_WIDTH)


def dsa_attention(q, c_kv, q_idx, k_idx, w_idx, positions, kv_norm_g, w_uk, w_uv, kidx_norm_g):
    B, S = q.shape[0], q.shape[1]
    top_k = min(TOPK_MAX, S // 4)
    c_kv = rms_norm(c_kv, kv_norm_g)
    k = rope((c_kv @ w_uk)[:, :, None, :], positions)[:, :, 0, :]
    v = c_kv @ w_uv
    q = rope(q, positions)
    q_idx = rope(q_idx, positions)
    k_idx = rope(rms_norm(k_idx, kidx_norm_g)[:, :, None, :], positions)[:, :, 0, :]
    w_idx = w_idx.astype(jnp.float32) * (IDX_HEADS ** -0.5) * (IDX_DIM ** -0.5)
    key_chunk = jnp.arange(S) // CHUNK

    def query_block(qb):
        start = qb * Q_BLOCK
        q_blk = lax.dynamic_slice_in_dim(q, start, Q_BLOCK, axis=1)
        qi_blk = lax.dynamic_slice_in_dim(q_idx, start, Q_BLOCK, axis=1)
        wi_blk = lax.dynamic_slice_in_dim(w_idx, start, Q_BLOCK, axis=1)
        q_chunk = (start + jnp.arange(Q_BLOCK)) // CHUNK
        admissible = key_chunk[None, :] <= q_chunk[:, None]
        idx_logits = jnp.einsum('bqhd,bsd->bqhs', qi_blk, k_idx).astype(jnp.float32)
        score = jnp.einsum('bqh,bqhs->bqs', wi_blk, jax.nn.relu(idx_logits))
        score = jnp.where(admissible[None], score, -jnp.inf)
        top_val, top_idx = lax.top_k(score, top_k)
        valid = jnp.isfinite(top_val)
        k_sel = jax.vmap(lambda kb, ib: kb[ib])(k, top_idx)
        v_sel = jax.vmap(lambda vb, ib: vb[ib])(v, top_idx)
        s = jnp.einsum('bqhd,bqkd->bqhk', q_blk, k_sel).astype(jnp.float32) * (HEAD_DIM ** -0.5)
        s = jnp.where(valid[:, :, None, :], s, -jnp.inf)
        p = jax.nn.softmax(s, axis=-1).astype(v.dtype)
        return jnp.einsum('bqhk,bqkd->bqhd', p, v_sel)

    out = lax.map(query_block, jnp.arange(S // Q_BLOCK))
    return out.transpose(1, 0, 2, 3, 4).reshape(B, S, ATT_WIDTH)


def hierarchical_moe(h, w_group, b_group, w_expert, b_expert, w1, w3, w2):
    T, D = h.shape
    hf = h.astype(jnp.float32)
    g_prob = jax.nn.softmax(hf @ w_group + b_group, axis=-1)
    g_p, g_idx = lax.top_k(g_prob, 1)
    e_logits = (hf @ w_expert + b_expert).reshape(T, N_EXPERT_GROUPS, EXPERTS_PER_GROUP)
    e_logits = jnp.take_along_axis(e_logits, g_idx[:, :, None], axis=1)[:, 0]
    e_prob = jax.nn.softmax(e_logits, axis=-1)
    e_p, e_idx = lax.top_k(e_prob, TOP_K_IN_GROUP)
    e_p = e_p / jnp.sum(e_p, axis=-1, keepdims=True)
    weights = g_p * e_p
    experts = g_idx * EXPERTS_PER_GROUP + e_idx

    M = T * TOP_K_IN_GROUP
    e_flat = experts.reshape(M)
    w_flat = weights.reshape(M).astype(h.dtype)
    tok_flat = jnp.arange(M, dtype=jnp.int32) // TOP_K_IN_GROUP
    order = jnp.argsort(e_flat)
    e_sorted = e_flat[order]
    counts = jnp.zeros((N_EXPERTS,), jnp.int32).at[e_flat].add(1)
    starts = jnp.cumsum(counts) - counts
    padded = (counts + MOE_BLOCK - 1) // MOE_BLOCK * MOE_BLOCK
    pends = jnp.cumsum(padded)
    pstarts = pends - padded
    dest = pstarts[e_sorted] + (jnp.arange(M, dtype=jnp.int32) - starts[e_sorted])
    n_blocks = M // MOE_BLOCK + N_EXPERTS
    n_slots = n_blocks * MOE_BLOCK
    slot_tok = jnp.zeros((n_slots,), jnp.int32).at[dest].set(tok_flat[order])
    slot_w = jnp.zeros((n_slots,), h.dtype).at[dest].set(w_flat[order])
    block_start = jnp.arange(n_blocks, dtype=jnp.int32) * MOE_BLOCK
    block_expert = jnp.minimum(jnp.searchsorted(pends, block_start, side='right'), N_EXPERTS - 1)

    def expert_block(args):
        e, toks, wts = args
        xb = h[toks]
        y = (jax.nn.silu(xb @ w1[e]) * (xb @ w3[e])) @ w2[e]
        return y * wts[:, None]

    ys = lax.map(expert_block, (block_expert,
                                slot_tok.reshape(n_blocks, MOE_BLOCK),
                                slot_w.reshape(n_blocks, MOE_BLOCK)))
    return jnp.zeros_like(h).at[slot_tok].add(ys.reshape(n_slots, D))


def setup_inputs(seed: int = 0) -> dict:
    key = jax.random.key(seed)
    ks = jax.random.split(key, 32)

    def nrm(k, shape, scale):
        return jax.random.normal(k, shape, jnp.float32) * scale

    L = DEPTH
    offsets = jax.random.randint(ks[2], (BATCH, 1), 0, 64, dtype=jnp.int32) * CHUNK
    return {
        "x": nrm(ks[0], (BATCH, SEQ, D_MODEL), 1.0),
        "c": nrm(ks[1], (BATCH, D_MODEL), 1.0),
        "positions": offsets + jnp.arange(SEQ, dtype=jnp.int32)[None, :],
        "w_ada": nrm(ks[3], (L, D_MODEL, 6 * D_MODEL), 0.5 * D_MODEL ** -0.5),
        "b_ada": nrm(ks[4], (L, 6 * D_MODEL), 0.02),
        "norm1_g": 1.0 + nrm(ks[5], (L, D_MODEL), 0.02),
        "w_in": nrm(ks[6], (L, D_MODEL, IN_COLS), D_MODEL ** -0.5),
        "v_norm_g": 1.0 + nrm(ks[7], (L, GMLP_GROUPS, GMLP_GROUP_DIM), 0.02),
        "v_norm_b": nrm(ks[8], (L, GMLP_GROUPS, GMLP_GROUP_DIM), 0.02),
        "w_sp": nrm(ks[9], (L, GMLP_GROUPS, GMLP_BLOCK, GMLP_BLOCK), GMLP_BLOCK ** -0.5),
        "b_sp": 1.0 + nrm(ks[10], (L, GMLP_GROUPS, GMLP_BLOCK), 0.02),
        "kv_norm_g": 1.0 + nrm(ks[11], (L, KV_LATENT), 0.02),
        "w_uk": nrm(ks[12], (L, KV_LATENT, HEAD_DIM), KV_LATENT ** -0.5),
        "w_uv": nrm(ks[13], (L, KV_LATENT, HEAD_DIM), KV_LATENT ** -0.5),
        "kidx_norm_g": 1.0 + nrm(ks[14], (L, IDX_DIM), 0.02),
        "gnorm_a_g": 1.0 + nrm(ks[15], (L, GMLP_WIDTH), 0.02),
        "gnorm_b_g": 1.0 + nrm(ks[16], (L, ATT_WIDTH), 0.02),
        "w_out": nrm(ks[17], (L, D_MIX, D_MODEL), D_MIX ** -0.5),
        "norm2_g": 1.0 + nrm(ks[18], (L, D_MODEL), 0.02),
        "w_group": nrm(ks[19], (L, D_MODEL, N_EXPERT_GROUPS), D_MODEL ** -0.5),
        "b_group": nrm(ks[20], (L, N_EXPERT_GROUPS), 0.01),
        "w_expert": nrm(ks[21], (L, D_MODEL, N_EXPERTS), D_MODEL ** -0.5),
        "b_expert": nrm(ks[22], (L, N_EXPERTS), 0.01),
        "w1": nrm(ks[23], (L, N_EXPERTS, D_MODEL, D_EXPERT), D_MODEL ** -0.5),
        "w3": nrm(ks[24], (L, N_EXPERTS, D_MODEL, D_EXPERT), D_MODEL ** -0.5),
        "w2": nrm(ks[25], (L, N_EXPERTS, D_EXPERT, D_MODEL), D_EXPERT ** -0.5),
        "final_g": 1.0 + nrm(ks[26], (D_MODEL,), 0.02),
    }


def reference(x, c, positions, w_ada, b_ada, norm1_g, w_in, v_norm_g, v_norm_b, w_sp, b_sp,
              kv_norm_g, w_uk, w_uv, kidx_norm_g, gnorm_a_g, gnorm_b_g, w_out, norm2_g,
              w_group, b_group, w_expert, b_expert, w1, w3, w2, final_g):
    B, S, D = x.shape
    split_points = [int(p) for p in np.cumsum(IN_SPLIT_SIZES)[:-1]]
    for l in range(DEPTH):
        mod = jax.nn.silu(c) @ w_ada[l] + b_ada[l]
        shift1, scale1, gate1, shift2, scale2, gate2 = [m[:, None, :] for m in jnp.split(mod, 6, axis=-1)]

        h = rms_norm(x, norm1_g[l]) * (1.0 + scale1) + shift1
        proj = h @ w_in[l]
        u, v, q, c_kv, q_idx, k_idx, w_idx = jnp.split(proj, split_points, axis=-1)
        y_a = gmlp_spatial_gate(jax.nn.gelu(u), jax.nn.gelu(v), v_norm_g[l], v_norm_b[l], w_sp[l], b_sp[l])
        y_b = dsa_attention(q.reshape(B, S, N_HEADS, HEAD_DIM), c_kv,
                            q_idx.reshape(B, S, IDX_HEADS, IDX_DIM), k_idx, w_idx, positions,
                            kv_norm_g[l], w_uk[l], w_uv[l], kidx_norm_g[l])
        y = jnp.concatenate([rms_norm(y_a, gnorm_a_g[l]), rms_norm(y_b, gnorm_b_g[l])], axis=-1)
        x = x + gate1 * (y @ w_out[l])

        h2 = rms_norm(x, norm2_g[l]) * (1.0 + scale2) + shift2
        moe = hierarchical_moe(h2.reshape(B * S, D), w_group[l], b_group[l], w_expert[l], b_expert[l],
                               w1[l], w3[l], w2[l])
        x = x + gate2 * moe.reshape(B, S, D)
    return rms_norm(x, final_g)
```

```python
import functools

import jax
import jax.numpy as jnp
from jax import lax
from jax.experimental import pallas as pl
from jax.experimental.pallas import tpu as pltpu

CHUNK = 64
EPS = 1e-6
GMLP_BLOCK = 128
GROUP_DIM = 128
HEAD_DIM = 64
KV_LATENT = 256
IDX_HEADS = 8
IDX_DIM = 64
TOPK_MAX = 256
ROPE_THETA = 10000.0
N_EXPERT_GROUPS = 4
EXPERTS_PER_GROUP = 8
N_EXPERTS = N_EXPERT_GROUPS * EXPERTS_PER_GROUP
LANES = 128

VMEM_LIMIT_BYTES = 56 * 1024 * 1024

INT_MIN = -(2 ** 31)
NEG_INF_KEY = (0xFF800000 ^ 0x7FFFFFFF) - (1 << 32)
MASK_VALUE = -1e30

F32 = jnp.float32
BF16 = jnp.bfloat16


def _gelu_tanh(x):
    return 0.5 * x * (1.0 + jnp.tanh(0.7978845608028654 * (x + 0.044715 * (x * x * x))))


def _dot(a, b):
    return jnp.dot(a, b, preferred_element_type=F32)


def _dot_nt(a, b):
    return lax.dot_general(a, b, (((1,), (1,)), ((), ())), preferred_element_type=F32)


def _ada_kernel(c_ref, w_ref, b_ref, o_ref):
    c = c_ref[...]
    sc = (c * jax.nn.sigmoid(c)).astype(BF16)
    o_ref[...] = _dot(sc, w_ref[...].astype(BF16)) + b_ref[...]


def _ada(c, w, b):
    B, D = c.shape
    N = w.shape[1]
    tn = 1024 if N % 1024 == 0 else N
    return pl.pallas_call(
        _ada_kernel,
        grid=(N // tn,),
        in_specs=[
            pl.BlockSpec((B, D), lambda j: (0, 0)),
            pl.BlockSpec((D, tn), lambda j: (0, j)),
            pl.BlockSpec((1, tn), lambda j: (0, j)),
        ],
        out_specs=pl.BlockSpec((B, tn), lambda j: (0, j)),
        out_shape=jax.ShapeDtypeStruct((B, N), F32),
        compiler_params=pltpu.CompilerParams(vmem_limit_bytes=VMEM_LIMIT_BYTES),
        name="ada",
    )(c, w, b.reshape(1, N))


def _rope_tab_kernel(pos_ref, freq_ref, cos_ref, sin_ref):
    ang = pos_ref[...] * freq_ref[...]
    cos_ref[...] = jnp.cos(ang)
    sin_ref[...] = jnp.sin(ang)


def _rope_tables(positions):
    half = HEAD_DIM // 2
    T = positions.size
    rep = LANES // half
    freqs = ROPE_THETA ** (-jnp.arange(half, dtype=F32) / half)
    pos = jnp.repeat(positions.reshape(T).astype(F32), half).reshape(T // rep, LANES)
    freq = jnp.tile(freqs, rep).reshape(1, LANES)
    R = T // rep
    tr = min(R, 1024)
    cos, sin = pl.pallas_call(
        _rope_tab_kernel,
        grid=(R // tr,),
        in_specs=[pl.BlockSpec((tr, LANES), lambda i: (i, 0)), pl.BlockSpec((1, LANES), lambda i: (0, 0))],
        out_specs=[pl.BlockSpec((tr, LANES), lambda i: (i, 0))] * 2,
        out_shape=[jax.ShapeDtypeStruct((R, LANES), F32)] * 2,
        name="rope_tab",
    )(pos, freq)
    cos4 = jnp.tile(cos.reshape(T, half), (1, rep))
    sin4 = jnp.tile(sin.reshape(T, half), (1, rep))
    return cos4, sin4


def _inproj_kernel(x_ref, mod_ref, g1_ref, wuv_ref, watt_ref, vg_ref, vb_ref, wsp_ref, bsp_ref,
                   kvg_ref, wukv_ref, kig_ref, ga_ref, cos_ref, sin_ref,
                   ya_ref, q_ref, qi_ref, k_ref, v_ref, ki_ref, wi_ref, ya_scr,
                   *, S, D, GW, AW, tm):
    i = pl.program_id(0)
    b = (i * tm) // S
    x = x_ref[...]
    ms = jnp.mean(x * x, axis=-1, keepdims=True)
    xn = x * lax.rsqrt(ms + EPS) * g1_ref[...]
    shift = mod_ref[pl.ds(b, 1), 0:D]
    scale = mod_ref[pl.ds(b, 1), D:2 * D]
    h = (xn * (1.0 + scale) + shift).astype(BF16)

    prow = lax.broadcasted_iota(jnp.int32, (GMLP_BLOCK, GMLP_BLOCK), 0)
    pcol = lax.broadcasted_iota(jnp.int32, (GMLP_BLOCK, GMLP_BLOCK), 1)
    causal = (prow // CHUNK) >= (pcol // CHUNK)
    n_groups = GW // GROUP_DIM
    n_rb = tm // GMLP_BLOCK
    sumsq = [jnp.zeros((GMLP_BLOCK, 1), F32) for _ in range(n_rb)]
    for gp in range(n_groups // 2):
        c0 = gp * 2 * GROUP_DIM
        u2 = _gelu_tanh(_dot(h, wuv_ref[:, c0:c0 + 2 * GROUP_DIM]))
        v2 = _gelu_tanh(_dot(h, wuv_ref[:, GW + c0:GW + c0 + 2 * GROUP_DIM]))
        for sub in range(2):
            g = gp * 2 + sub
            lo = sub * GROUP_DIM
            gu = u2[:, lo:lo + GROUP_DIM]
            gv = v2[:, lo:lo + GROUP_DIM]
            mu = jnp.mean(gv, axis=-1, keepdims=True)
            dv = gv - mu
            var = jnp.mean(dv * dv, axis=-1, keepdims=True)
            gc = g * GROUP_DIM
            vn = (dv * lax.rsqrt(var + EPS) * vg_ref[:, gc:gc + GROUP_DIM] + vb_ref[:, gc:gc + GROUP_DIM]).astype(BF16)
            wm = jnp.where(causal, wsp_ref[g], 0.0).astype(BF16)
            bias = bsp_ref[:, g:g + 1]
            for r in range(n_rb):
                r0 = r * GMLP_BLOCK
                z = _dot(wm, vn[r0:r0 + GMLP_BLOCK, :]) + bias
                ya = gu[r0:r0 + GMLP_BLOCK, :] * z
                ya_scr[r0:r0 + GMLP_BLOCK, gc:gc + GROUP_DIM] = ya
                sumsq[r] = sumsq[r] + jnp.sum(ya * ya, axis=-1, keepdims=True)
    for r in range(n_rb):
        r0 = r * GMLP_BLOCK
        inv = lax.rsqrt(sumsq[r] * (1.0 / GW) + EPS)
        ya_ref[r0:r0 + GMLP_BLOCK, :] = (ya_scr[r0:r0 + GMLP_BLOCK, :] * inv * ga_ref[...]).astype(BF16)

    lane = lax.broadcasted_iota(jnp.int32, (tm, LANES), 1)
    first_half = (lane % HEAD_DIM) < (HEAD_DIM // 2)
    cos = cos_ref[...]
    sin_s = jnp.where(first_half, -sin_ref[...], sin_ref[...])

    def rope(t):
        swapped = jnp.where(first_half, pltpu.roll(t, LANES - HEAD_DIM // 2, 1), pltpu.roll(t, HEAD_DIM // 2, 1))
        return t * cos + swapped * sin_s

    for j in range(AW // 256):
        qq = _dot(h, watt_ref[:, 256 * j:256 * (j + 1)])
        for sub in range(2):
            c0 = 256 * j + LANES * sub
            q_ref[:, c0:c0 + LANES] = (rope(qq[:, LANES * sub:LANES * (sub + 1)]) * (HEAD_DIM ** -0.5)).astype(BF16)

    off = AW
    ck = _dot(h, watt_ref[:, off:off + KV_LATENT])
    ckn = (ck * lax.rsqrt(jnp.mean(ck * ck, axis=-1, keepdims=True) + EPS) * kvg_ref[...]).astype(BF16)
    kv = _dot(ckn, wukv_ref[...])
    k_ref[...] = rope(kv)[:, 0:HEAD_DIM].astype(BF16)
    v_ref[...] = kv[:, HEAD_DIM:2 * HEAD_DIM].astype(BF16)

    off = AW + KV_LATENT
    for j in range(IDX_HEADS * IDX_DIM // 256):
        qq = _dot(h, watt_ref[:, off + 256 * j:off + 256 * (j + 1)])
        for sub in range(2):
            c0 = 256 * j + LANES * sub
            qi_ref[:, c0:c0 + LANES] = rope(qq[:, LANES * sub:LANES * (sub + 1)]).astype(BF16)

    off = AW + KV_LATENT + IDX_HEADS * IDX_DIM
    t = _dot(h, watt_ref[:, off:off + LANES])
    is_ki = lane < IDX_DIM
    kms = jnp.sum(jnp.where(is_ki, t * t, 0.0), axis=-1, keepdims=True) * (1.0 / IDX_DIM)
    kin = t * lax.rsqrt(kms + EPS) * kig_ref[...]
    ki_ref[...] = rope(kin)[:, 0:IDX_DIM].astype(BF16)
    wi_ref[...] = t[:, IDX_DIM:IDX_DIM + IDX_HEADS] * (IDX_HEADS ** -0.5) * (IDX_DIM ** -0.5)


def _inproj(x2, mod, norm1_g, w_in, v_norm_g, v_norm_b, w_sp, b_sp, kv_norm_g, w_uk, w_uv,
            kidx_norm_g, gnorm_a_g, cos4, sin4, *, S):
    T, D = x2.shape
    GW = D // 2
    AW = D - GW
    B = mod.shape[0]
    tm = 256
    n_groups = GW // GROUP_DIM
    w_in_b = w_in.astype(BF16)
    wuv = w_in_b[:, :2 * GW]
    att_cols = AW + KV_LATENT + IDX_HEADS * IDX_DIM + LANES
    watt = jnp.pad(w_in_b[:, 2 * GW:], ((0, 0), (0, att_cols - (w_in.shape[1] - 2 * GW))))
    wukv = jnp.concatenate([w_uk, w_uv], axis=1).astype(BF16)
    kig = jnp.pad(kidx_norm_g, (0, LANES - IDX_DIM)).reshape(1, LANES)
    full = lambda shape: pl.BlockSpec(shape, lambda i: (0,) * len(shape))
    row = lambda w: pl.BlockSpec((tm, w), lambda i: (i, 0))
    kernel = functools.partial(_inproj_kernel, S=S, D=D, GW=GW, AW=AW, tm=tm)
    return pl.pallas_call(
        kernel,
        grid=(T // tm,),
        in_specs=[
            row(D), full((B, 6 * D)), full((1, D)), full((D, 2 * GW)), full((D, att_cols)),
            full((1, GW)), full((1, GW)), full((n_groups, GMLP_BLOCK, GMLP_BLOCK)), full((GMLP_BLOCK, n_groups)),
            full((1, KV_LATENT)), full((KV_LATENT, 2 * HEAD_DIM)), full((1, LANES)), full((1, GW)),
            row(LANES), row(LANES),
        ],
        out_specs=[row(GW), row(AW), row(IDX_HEADS * IDX_DIM), row(HEAD_DIM), row(HEAD_DIM), row(IDX_DIM),
                   row(IDX_HEADS)],
        out_shape=[
            jax.ShapeDtypeStruct((T, GW), BF16), jax.ShapeDtypeStruct((T, AW), BF16),
            jax.ShapeDtypeStruct((T, IDX_HEADS * IDX_DIM), BF16), jax.ShapeDtypeStruct((T, HEAD_DIM), BF16),
            jax.ShapeDtypeStruct((T, HEAD_DIM), BF16), jax.ShapeDtypeStruct((T, IDX_DIM), BF16),
            jax.ShapeDtypeStruct((T, IDX_HEADS), F32),
        ],
        scratch_shapes=[pltpu.VMEM((tm, GW), F32)],
        compiler_params=pltpu.CompilerParams(vmem_limit_bytes=VMEM_LIMIT_BYTES),
        name="inproj",
    )(x2, mod, norm1_g.reshape(1, D), wuv, watt, v_norm_g.reshape(1, GW), v_norm_b.reshape(1, GW), w_sp,
      b_sp.T, kv_norm_g.reshape(1, KV_LATENT), wukv, kig, gnorm_a_g.reshape(1, GW), cos4, sin4)


def _sort_key(s):
    bits = pltpu.bitcast(s, jnp.int32)
    return jnp.where(bits < 0, bits ^ 0x7FFFFFFF, bits)


def _dsa_kernel(q_ref, qi_ref, wi_ref, ki_ref, k_ref, v_ref, gb_ref, o_ref,
                key_scr, cut_scr, m_scr, l_scr, acc_scr, y_scr, *, tq, tk, S, AW, top_k):
    qt = pl.program_id(1)
    q0 = qt * tq
    nk = (q0 + tq + tk - 1) // tk
    n_heads = AW // HEAD_DIM
    qchunk = (q0 + lax.broadcasted_iota(jnp.int32, (tq, 1), 0)) // CHUNK
    lane_k = lax.broadcasted_iota(jnp.int32, (tq, tk), 1)

    def score_body(c, carry):
        ks = pl.multiple_of(c * tk, tk)
        kc = ki_ref[pl.ds(ks, tk), :]
        acc = jnp.zeros((tq, tk), F32)
        for hh in range(IDX_HEADS):
            a = _dot_nt(qi_ref[:, hh * IDX_DIM:(hh + 1) * IDX_DIM], kc)
            acc = acc + wi_ref[:, hh:hh + 1] * jnp.maximum(a, 0.0)
        adm = ((ks + lane_k) // CHUNK) <= qchunk
        key_scr[:, pl.ds(ks, tk)] = _sort_key(jnp.where(adm, acc, -jnp.inf))
        return carry

    lax.fori_loop(0, nk, score_body, 0)

    def count(pred):
        def body(c, acc):
            ks = pl.multiple_of(c * tk, tk)
            m = pred(key_scr[:, pl.ds(ks, tk)], ks).astype(jnp.int32)
            for j in range(tk // LANES):
                acc = acc + m[:, j * LANES:(j + 1) * LANES]
            return acc
        acc = lax.fori_loop(0, nk, body, jnp.zeros((tq, LANES), jnp.int32))
        return jnp.sum(acc, axis=1, keepdims=True)

    zero = jnp.zeros((tq, 1), jnp.int32)
    cnt0 = count(lambda kk, ks: kk >= zero)
    ans = jnp.where(cnt0 >= top_k, zero, zero + INT_MIN)

    def bit_body(it, ans):
        cand = ans + jnp.left_shift(jnp.int32(1), 30 - it)
        cnt = count(lambda kk, ks: kk >= cand)
        return jnp.where(cnt >= top_k, cand, ans)

    ans = lax.fori_loop(0, 31, bit_body, ans)
    n_gt = count(lambda kk, ks: kk > ans)
    n_eq = count(lambda kk, ks: kk == ans)
    need = top_k - n_gt
    surplus = (ans > NEG_INF_KEY) & (n_eq > need)

    cut_scr[...] = jnp.full((tq, LANES), S, jnp.int32)

    @pl.when(jnp.max(surplus.astype(jnp.int32)) > 0)
    def _():
        cp = zero
        for bit in reversed(range(max(S - 1, 1).bit_length())):
            cand = cp + (1 << bit)
            cnt = count(lambda kk, ks: (kk == ans) & ((ks + lane_k) < cand))
            cp = jnp.where(cnt < need, cand, cp)
        cut_scr[...] = jnp.broadcast_to(jnp.where(surplus, cp + 1, S), (tq, LANES))

    cut = cut_scr[:, 0:1]

    m_scr[...] = jnp.full(m_scr.shape, MASK_VALUE, F32)
    l_scr[...] = jnp.zeros(l_scr.shape, F32)
    acc_scr[...] = jnp.zeros(acc_scr.shape, F32)

    def attn_body(c, carry):
        ks = pl.multiple_of(c * tk, tk)
        kk = key_scr[:, pl.ds(ks, tk)]
        sel = (kk > NEG_INF_KEY) & ((kk > ans) | ((kk == ans) & ((ks + lane_k) < cut)))
        kc = k_ref[pl.ds(ks, tk), :]
        vc = v_ref[pl.ds(ks, tk), :]
        for hh in range(n_heads):
            s = _dot_nt(q_ref[:, hh * HEAD_DIM:(hh + 1) * HEAD_DIM], kc)
            s = jnp.where(sel, s, MASK_VALUE)
            m_old = m_scr[hh]
            m_new = jnp.maximum(m_old, jnp.max(s, axis=1, keepdims=True))
            alpha = jnp.exp(m_old - m_new)
            p = jnp.exp(s - m_new[:, 0:1])
            l_scr[hh] = alpha * l_scr[hh] + jnp.sum(p, axis=1, keepdims=True)
            acc_scr[hh] = acc_scr[hh] * alpha[:, 0:HEAD_DIM] + _dot(p.astype(BF16), vc)
            m_scr[hh] = m_new
        return carry

    lax.fori_loop(0, nk, attn_body, 0)

    for hh in range(n_heads):
        y_scr[:, hh * HEAD_DIM:(hh + 1) * HEAD_DIM] = acc_scr[hh] / l_scr[hh][:, 0:HEAD_DIM]
    y = y_scr[...]
    inv = lax.rsqrt(jnp.mean(y * y, axis=-1, keepdims=True) + EPS)
    o_ref[...] = (y * inv * gb_ref[...]).astype(BF16)


def _dsa(q, qi, wi, ki, k, v, gnorm_b_g, *, B, S):
    AW = q.shape[-1]
    n_heads = AW // HEAD_DIM
    tq = 128
    tk = 256 if S % 256 == 0 else 128
    top_k = min(TOPK_MAX, S // 4)
    qspec = lambda w: pl.BlockSpec((None, tq, w), lambda b, t: (b, t, 0))
    kspec = lambda w: pl.BlockSpec((None, S, w), lambda b, t: (b, 0, 0))
    kernel = functools.partial(_dsa_kernel, tq=tq, tk=tk, S=S, AW=AW, top_k=top_k)
    r3 = lambda a: a.reshape(B, S, a.shape[-1])
    out = pl.pallas_call(
        kernel,
        grid=(B, S // tq),
        in_specs=[qspec(AW), qspec(IDX_HEADS * IDX_DIM), qspec(IDX_HEADS), kspec(IDX_DIM), kspec(HEAD_DIM),
                  kspec(HEAD_DIM), pl.BlockSpec((1, AW), lambda b, t: (0, 0))],
        out_specs=qspec(AW),
        out_shape=jax.ShapeDtypeStruct((B, S, AW), BF16),
        scratch_shapes=[
            pltpu.VMEM((tq, S), jnp.int32), pltpu.VMEM((tq, LANES), jnp.int32),
            pltpu.VMEM((n_heads, tq, LANES), F32), pltpu.VMEM((n_heads, tq, LANES), F32),
            pltpu.VMEM((n_heads, tq, HEAD_DIM), F32), pltpu.VMEM((tq, AW), F32),
        ],
        compiler_params=pltpu.CompilerParams(vmem_limit_bytes=VMEM_LIMIT_BYTES),
        name="dsa",
    )(r3(q), r3(qi), r3(wi), r3(ki), r3(k), r3(v), gnorm_b_g.reshape(1, AW))
    return out.reshape(B * S, AW)


ROUTE_E0, ROUTE_E1, ROUTE_R0, ROUTE_R1, ROUTE_W0, ROUTE_W1 = range(6)


def _outproj_kernel(ya_ref, yb_ref, x_ref, mod_ref, wo_ref, g2_ref, wrt_ref, brt_ref,
                    x1_ref, h2_ref, route_ref, cnt_ref, cnt_scr, *, S, D, GW, tm):
    i = pl.program_id(0)
    b = (i * tm) // S

    @pl.when(i == 0)
    def _():
        cnt_scr[...] = jnp.zeros(cnt_scr.shape, F32)

    y = _dot(ya_ref[...], wo_ref[0:GW, :]) + _dot(yb_ref[...], wo_ref[GW:D, :])
    gate1 = mod_ref[pl.ds(b, 1), 2 * D:3 * D]
    x1 = x_ref[...] + gate1 * y
    x1_ref[...] = x1
    shift2 = mod_ref[pl.ds(b, 1), 3 * D:4 * D]
    scale2 = mod_ref[pl.ds(b, 1), 4 * D:5 * D]
    inv = lax.rsqrt(jnp.mean(x1 * x1, axis=-1, keepdims=True) + EPS)
    h2 = x1 * inv * g2_ref[...] * (1.0 + scale2) + shift2
    h2_ref[...] = h2

    logits = _dot(h2.astype(BF16), wrt_ref[...]) + brt_ref[...]
    lane = lax.broadcasted_iota(jnp.int32, (tm, LANES), 1)
    big = jnp.int32(LANES)
    is_g = lane < N_EXPERT_GROUPS
    gl = jnp.where(is_g, logits, -jnp.inf)
    gmax = jnp.max(gl, axis=1, keepdims=True)
    gidx = jnp.min(jnp.where(gl == gmax, lane, big), axis=1, keepdims=True)
    g_p = 1.0 / jnp.sum(jnp.where(is_g, jnp.exp(logits - gmax), 0.0), axis=1, keepdims=True)
    lo = N_EXPERT_GROUPS + EXPERTS_PER_GROUP * gidx
    el = jnp.where((lane >= lo) & (lane < lo + EXPERTS_PER_GROUP), logits, -jnp.inf)
    e1 = jnp.max(el, axis=1, keepdims=True)
    i1 = jnp.min(jnp.where(el == e1, lane, big), axis=1, keepdims=True)
    el2 = jnp.where(lane == i1, -jnp.inf, el)
    e2 = jnp.max(el2, axis=1, keepdims=True)
    i2 = jnp.min(jnp.where(el2 == e2, lane, big), axis=1, keepdims=True)
    t = jnp.exp(e2 - e1)
    w0 = g_p * (1.0 / (1.0 + t))
    w1 = g_p * (t / (1.0 + t))

    hit0 = lane == i1
    hit1 = lane == i2
    onehot = jnp.where(hit0 | hit1, 1.0, 0.0)
    r_i = lax.broadcasted_iota(jnp.int32, (tm, tm), 0)
    c_i = lax.broadcasted_iota(jnp.int32, (tm, tm), 1)
    lower = jnp.where(c_i < r_i, 1.0, 0.0).astype(BF16)
    before = _dot(lower, onehot.astype(BF16)) + cnt_scr[0:1, :]
    rank0 = jnp.sum(jnp.where(hit0, before, 0.0), axis=1, keepdims=True)
    rank1 = jnp.sum(jnp.where(hit1, before, 0.0), axis=1, keepdims=True)
    cnt_new = cnt_scr[0:1, :] + jnp.sum(onehot, axis=0, keepdims=True)
    cnt_scr[0:1, :] = cnt_new
    cnt_ref[...] = jnp.broadcast_to(cnt_new, cnt_ref.shape)

    e0f = (i1 - N_EXPERT_GROUPS).astype(F32)
    e1f = (i2 - N_EXPERT_GROUPS).astype(F32)
    route = jnp.zeros((tm, LANES), F32)
    for idx, val in ((ROUTE_E0, e0f), (ROUTE_E1, e1f), (ROUTE_R0, rank0), (ROUTE_R1, rank1),
                     (ROUTE_W0, w0), (ROUTE_W1, w1)):
        route = jnp.where(lane == idx, val, route)
    route_ref[...] = route


def _outproj(ya, yb, x2, mod, w_out, norm2_g, w_group, b_group, w_expert, b_expert, *, S):
    T, D = x2.shape
    GW = ya.shape[1]
    B = mod.shape[0]
    tm = 256
    wrt = jnp.pad(jnp.concatenate([w_group, w_expert], axis=1),
                  ((0, 0), (0, LANES - N_EXPERT_GROUPS - N_EXPERTS))).astype(BF16)
    brt = jnp.pad(jnp.concatenate([b_group, b_expert]), (0, LANES - N_EXPERT_GROUPS - N_EXPERTS)).reshape(1, LANES)
    full = lambda shape: pl.BlockSpec(shape, lambda i: (0,) * len(shape))
    row = lambda w: pl.BlockSpec((tm, w), lambda i: (i, 0))
    kernel = functools.partial(_outproj_kernel, S=S, D=D, GW=GW, tm=tm)
    return pl.pallas_call(
        kernel,
        grid=(T // tm,),
        in_specs=[row(GW), row(D - GW), row(D), full((B, 6 * D)), full((D, D)), full((1, D)), full((D, LANES)),
                  full((1, LANES))],
        out_specs=[row(D), row(D), row(LANES), full((8, LANES))],
        out_shape=[jax.ShapeDtypeStruct((T, D), F32), jax.ShapeDtypeStruct((T, D), F32),
                   jax.ShapeDtypeStruct((T, LANES), F32), jax.ShapeDtypeStruct((8, LANES), F32)],
        scratch_shapes=[pltpu.VMEM((8, LANES), F32)],
        compiler_params=pltpu.CompilerParams(dimension_semantics=("arbitrary",), vmem_limit_bytes=VMEM_LIMIT_BYTES),
        name="outproj",
    )(ya, yb, x2, mod, w_out.astype(BF16), norm2_g.reshape(1, D), wrt, brt)


MOE_ROWS = 512
ZERO_ROWS = 64


def _dispatch_kernel(dest_ref, pad_start_ref, pad_len_ref, nused_ref, h2_ref, xs_ref, zero_scr, sem, psem, tsem,
                     *, tr, n_blocks):
    i = pl.program_id(0)

    def row_copy(r, k):
        return pltpu.make_async_copy(h2_ref.at[pl.ds(r, 1)], xs_ref.at[pl.ds(dest_ref[0, 0, 2 * r + k], 1)], sem)

    def start_body(r, carry):
        row_copy(r, 0).start()
        row_copy(r, 1).start()
        return carry

    lax.fori_loop(0, tr, start_body, 0)

    def pad_copy(e, j):
        return pltpu.make_async_copy(zero_scr.at[pl.ds(0, 1)], xs_ref.at[pl.ds(pad_start_ref[e] + j, 1)], psem)

    def tail_copy(blk, j):
        start = pl.multiple_of(blk * MOE_ROWS + j * ZERO_ROWS, ZERO_ROWS)
        return pltpu.make_async_copy(zero_scr, xs_ref.at[pl.ds(start, ZERO_ROWS)], tsem)

    def for_each_pad_row(fn):
        def pad_body(e, carry):
            def one(j, carry):
                fn(pad_copy(e, j))
                return carry
            return lax.fori_loop(0, pad_len_ref[e], one, carry)
        lax.fori_loop(0, N_EXPERTS, pad_body, 0)

    def for_each_tail_chunk(fn):
        def tail_body(blk, carry):
            for j in range(MOE_ROWS // ZERO_ROWS):
                fn(tail_copy(blk, j))
            return carry
        lax.fori_loop(nused_ref[0], n_blocks, tail_body, 0)

    @pl.when(i == 0)
    def _():
        zero_scr[...] = jnp.zeros(zero_scr.shape, zero_scr.dtype)
        for_each_pad_row(lambda cp: cp.start())
        for_each_tail_chunk(lambda cp: cp.start())
        for_each_pad_row(lambda cp: cp.wait())
        for_each_tail_chunk(lambda cp: cp.wait())

    def wait_body(r, carry):
        row_copy(r, 0).wait()
        row_copy(r, 1).wait()
        return carry

    lax.fori_loop(0, tr, wait_body, 0)


def _dispatch(h2, dest, pad_start, pad_len, n_used, n_slots):
    T, D = h2.shape
    tr = 256
    nb = T // tr
    kernel = functools.partial(_dispatch_kernel, tr=tr, n_blocks=n_slots // MOE_ROWS)
    smem = pl.BlockSpec(memory_space=pltpu.SMEM)
    return pl.pallas_call(
        kernel,
        grid_spec=pltpu.PrefetchScalarGridSpec(
            num_scalar_prefetch=0,
            grid=(nb,),
            in_specs=[
                pl.BlockSpec((1, 1, 2 * tr), lambda i: (i, 0, 0), memory_space=pltpu.SMEM),
                smem, smem, smem,
                pl.BlockSpec((tr, D), lambda i: (i, 0)),
            ],
            out_specs=pl.BlockSpec(memory_space=pl.ANY),
            scratch_shapes=[pltpu.VMEM((ZERO_ROWS, D), F32)] + [pltpu.SemaphoreType.DMA] * 3,
        ),
        out_shape=jax.ShapeDtypeStruct((n_slots, D), F32),
        compiler_params=pltpu.CompilerParams(dimension_semantics=("arbitrary",), has_side_effects=True,
                                             vmem_limit_bytes=VMEM_LIMIT_BYTES),
        name="dispatch",
    )(dest.reshape(nb, 1, 2 * tr), pad_start, pad_len, n_used, h2)


def _moe_kernel(be_ref, nused_ref, xs_ref, w1_ref, w3_ref, w2_ref, ys_ref, acc_scr):
    i = pl.program_id(0)
    f = pl.program_id(1)

    @pl.when(i < nused_ref[0])
    def _():
        x = xs_ref[...].astype(BF16)
        a = _dot(x, w1_ref[...].astype(BF16))
        g = _dot(x, w3_ref[...].astype(BF16))
        hidden = (a * jax.nn.sigmoid(a) * g).astype(BF16)
        y = _dot(hidden, w2_ref[...].astype(BF16))

        @pl.when(f == 0)
        def _():
            acc_scr[...] = y

        @pl.when(f > 0)
        def _():
            acc_scr[...] += y

        @pl.when(f == pl.num_programs(1) - 1)
        def _():
            ys_ref[...] = acc_scr[...]

    @pl.when(i >= nused_ref[0])
    def _():
        ys_ref[...] = jnp.zeros(ys_ref.shape, ys_ref.dtype)


def _moe(xs, block_expert, n_used, w1, w3, w2):
    n_slots, D = xs.shape
    F = w1.shape[-1]
    bm = MOE_ROWS
    tf = 256
    nb = n_slots // bm
    blk = lambda i, be, nu: jnp.minimum(i, nu[0] - 1)
    return pl.pallas_call(
        _moe_kernel,
        grid_spec=pltpu.PrefetchScalarGridSpec(
            num_scalar_prefetch=2,
            grid=(nb, F // tf),
            in_specs=[
                pl.BlockSpec((bm, D), lambda i, f, be, nu: (blk(i, be, nu), 0)),
                pl.BlockSpec((None, D, tf), lambda i, f, be, nu: (be[blk(i, be, nu)], 0, f)),
                pl.BlockSpec((None, D, tf), lambda i, f, be, nu: (be[blk(i, be, nu)], 0, f)),
                pl.BlockSpec((None, tf, D), lambda i, f, be, nu: (be[blk(i, be, nu)], f, 0)),
            ],
            out_specs=pl.BlockSpec((bm, D), lambda i, f, be, nu: (i, 0)),
            scratch_shapes=[pltpu.VMEM((bm, D), F32)],
        ),
        out_shape=jax.ShapeDtypeStruct((n_slots, D), F32),
        compiler_params=pltpu.CompilerParams(dimension_semantics=("arbitrary", "arbitrary"),
                                             vmem_limit_bytes=VMEM_LIMIT_BYTES),
        name="moe",
    )(block_expert, n_used, xs, w1, w3, w2)


def _final_kernel(dest_ref, ys_ref, route_ref, x1_ref, mod_ref, fg_ref, o_ref, buf0, buf1, sem, *, S, D, tr):
    i = pl.program_id(0)
    b = (i * tr) // S

    def row_copy(r, k):
        buf = buf0 if k == 0 else buf1
        return pltpu.make_async_copy(ys_ref.at[pl.ds(dest_ref[0, 0, 2 * r + k], 1)], buf.at[pl.ds(r, 1)], sem)

    def start_body(r, carry):
        row_copy(r, 0).start()
        row_copy(r, 1).start()
        return carry

    lax.fori_loop(0, tr, start_body, 0)

    def wait_body(r, carry):
        row_copy(r, 0).wait()
        row_copy(r, 1).wait()
        return carry

    lax.fori_loop(0, tr, wait_body, 0)

    w0 = route_ref[:, ROUTE_W0:ROUTE_W0 + 1]
    w1 = route_ref[:, ROUTE_W1:ROUTE_W1 + 1]
    moe = buf0[...] * w0 + buf1[...] * w1
    gate2 = mod_ref[pl.ds(b, 1), 5 * D:6 * D]
    x2 = x1_ref[...] + gate2 * moe
    inv = lax.rsqrt(jnp.mean(x2 * x2, axis=-1, keepdims=True) + EPS)
    o_ref[...] = x2 * inv * fg_ref[...]


def _final(dest, ys, route, x1, mod, final_g, *, S):
    T, D = x1.shape
    B = mod.shape[0]
    tr = 256
    nb = T // tr
    kernel = functools.partial(_final_kernel, S=S, D=D, tr=tr)
    return pl.pallas_call(
        kernel,
        grid_spec=pltpu.PrefetchScalarGridSpec(
            num_scalar_prefetch=0,
            grid=(nb,),
            in_specs=[
                pl.BlockSpec((1, 1, 2 * tr), lambda i: (i, 0, 0), memory_space=pltpu.SMEM),
                pl.BlockSpec(memory_space=pl.ANY),
                pl.BlockSpec((tr, LANES), lambda i: (i, 0)),
                pl.BlockSpec((tr, D), lambda i: (i, 0)),
                pl.BlockSpec((B, 6 * D), lambda i: (0, 0)),
                pl.BlockSpec((1, D), lambda i: (0, 0)),
            ],
            out_specs=pl.BlockSpec((tr, D), lambda i: (i, 0)),
            scratch_shapes=[pltpu.VMEM((tr, D), F32), pltpu.VMEM((tr, D), F32), pltpu.SemaphoreType.DMA],
        ),
        out_shape=jax.ShapeDtypeStruct((T, D), F32),
        compiler_params=pltpu.CompilerParams(dimension_semantics=("arbitrary",),
                                             vmem_limit_bytes=VMEM_LIMIT_BYTES),
        name="final",
    )(dest.reshape(nb, 1, 2 * tr), ys, route, x1, mod, final_g.reshape(1, D))


def _layer(x2, c, positions, w_ada, b_ada, norm1_g, w_in, v_norm_g, v_norm_b, w_sp, b_sp, kv_norm_g, w_uk, w_uv,
           kidx_norm_g, gnorm_a_g, gnorm_b_g, w_out, norm2_g, w_group, b_group, w_expert, b_expert, w1, w3, w2,
           out_g, *, B, S):
    T, D = x2.shape
    mod = _ada(c, w_ada, b_ada)
    cos4, sin4 = _rope_tables(positions)
    ya, q, qi, k, v, ki, wi = _inproj(x2, mod, norm1_g, w_in, v_norm_g, v_norm_b, w_sp, b_sp, kv_norm_g, w_uk,
                                      w_uv, kidx_norm_g, gnorm_a_g, cos4, sin4, S=S)
    yb = _dsa(q, qi, wi, ki, k, v, gnorm_b_g, B=B, S=S)
    x1, h2, route, cnt = _outproj(ya, yb, x2, mod, w_out, norm2_g, w_group, b_group, w_expert, b_expert, S=S)

    bm = MOE_ROWS
    counts = cnt[0, N_EXPERT_GROUPS:N_EXPERT_GROUPS + N_EXPERTS].astype(jnp.int32)
    padded = (counts + bm - 1) // bm * bm
    pends = jnp.cumsum(padded)
    pstarts = pends - padded
    n_blocks = (2 * T) // bm + N_EXPERTS
    n_slots = n_blocks * bm
    block_start = jnp.arange(n_blocks, dtype=jnp.int32) * bm
    block_expert = jnp.minimum(jnp.searchsorted(pends, block_start, side="right"), N_EXPERTS - 1).astype(jnp.int32)
    n_used = (pends[-1:] // bm).astype(jnp.int32)
    experts = route[:, ROUTE_E0:ROUTE_E1 + 1].astype(jnp.int32)
    ranks = route[:, ROUTE_R0:ROUTE_R1 + 1].astype(jnp.int32)
    dest = (pstarts[experts] + ranks).reshape(2 * T)

    xs = _dispatch(h2, dest, (pstarts + counts).astype(jnp.int32), (padded - counts).astype(jnp.int32), n_used,
                   n_slots)
    ys = _moe(xs, block_expert, n_used, w1, w3, w2)
    return _final(dest, ys, route, x1, mod, out_g, S=S)


def kernel(x, c, positions, w_ada, b_ada, norm1_g, w_in, v_norm_g, v_norm_b, w_sp, b_sp, kv_norm_g, w_uk, w_uv,
           kidx_norm_g, gnorm_a_g, gnorm_b_g, w_out, norm2_g, w_group, b_group, w_expert, b_expert, w1, w3, w2,
           final_g):
    B, S, D = x.shape
    depth = w_ada.shape[0]
    assert depth == 1, "the final norm is fused into the last layer's combine kernel"
    out = _layer(x.reshape(B * S, D), c, positions, w_ada[0], b_ada[0], norm1_g[0], w_in[0], v_norm_g[0],
                 v_norm_b[0], w_sp[0], b_sp[0], kv_norm_g[0], w_uk[0], w_uv[0], kidx_norm_g[0], gnorm_a_g[0],
                 gnorm_b_g[0], w_out[0], norm2_g[0], w_group[0], b_group[0], w_expert[0], b_expert[0], w1[0],
                 w3[0], w2[0], final_g, B=B, S=S)
    return out.reshape(B, S, D)
```
